```python
import jax, jax.numpy as jnp
from jax import lax
import numpy as np

D_MODEL = 1024
BATCH = 8
SEQ = 4096
DEPTH = 4
DEC_BATCH = 8
DEC_SEQ = 64
PAST_LEN = 2048

CHUNK = 64
N_MIXERS = 2
N_LAYERS_A = (DEPTH + 1) // 2
N_LAYERS_B = DEPTH // 2
HEAD_DIM = 64
N_HEADS_A = 16
N_KV_A = 4
GQA_REP = N_HEADS_A // N_KV_A
N_IDX_HEADS = 8
IDX_DIM = 64
IDX_ROPE_DIM = 32
TOPK_CAP = 256
N_HEADS_B = 16
D_FF = -(-8 * D_MODEL // (3 * 256)) * 256
ROPE_THETA = 10000.0
QBLOCK = 128
EPS = 1e-6
FORGET_BIAS_MEAN = 3.0
A_WIDTHS = (N_HEADS_A * HEAD_DIM, N_KV_A * HEAD_DIM, N_KV_A * HEAD_DIM, N_IDX_HEADS * IDX_DIM, IDX_DIM, N_IDX_HEADS)
A_IN = sum(A_WIDTHS)
B_INNER = N_HEADS_B * HEAD_DIM
B_WIDTHS = (B_INNER, B_INNER, B_INNER, B_INNER, N_HEADS_B)
B_IN = sum(B_WIDTHS)

kernel_name = "dsa_fox_streaming_encoder_step"


def split_cols(x, widths):
    outs, off = [], 0
    for w in widths:
        outs.append(x[..., off:off + w])
        off += w
    return outs


def rms_norm(x, g):
    xf = x.astype(jnp.float32)
    y = xf * lax.rsqrt(jnp.mean(xf * xf, axis=-1, keepdims=True) + EPS)
    return (y * g.astype(jnp.float32)).astype(x.dtype)


def ada_modulate(h, shift, scale):
    return h * (1.0 + scale[:, None, :]) + shift[:, None, :]


def rope(x, pos):
    half = x.shape[-1] // 2
    inv = ROPE_THETA ** (-jnp.arange(half, dtype=jnp.float32) / half)
    ang = pos.astype(jnp.float32)[:, None] * inv[None, :]
    cos = jnp.cos(ang)[:, None, :]
    sin = jnp.sin(ang)[:, None, :]
    xf = x.astype(jnp.float32)
    x1, x2 = xf[..., :half], xf[..., half:]
    return jnp.concatenate([x1 * cos - x2 * sin, x2 * cos + x1 * sin], axis=-1).astype(x.dtype)


def partial_rope(x, pos):
    return jnp.concatenate([rope(x[..., :IDX_ROPE_DIM], pos), x[..., IDX_ROPE_DIM:]], axis=-1)


def sweep_queries(fn, q_pos, *qs):
    tq = q_pos.shape[0]
    qb = min(QBLOCK, tq)
    nb = tq // qb

    def split(a):
        return jnp.moveaxis(a.reshape(a.shape[0], nb, qb, *a.shape[2:]), 1, 0)

    out = lax.map(lambda blk: fn(*blk), (q_pos.reshape(nb, qb), *[split(a) for a in qs]))
    out = jnp.moveaxis(out, 0, 1)
    return out.reshape(out.shape[0], tq, *out.shape[3:])


def dsa_mixer(h, pos, w_in, w_out, past):
    b, t, _ = h.shape
    q, k, v, qi, ki, wi = split_cols(h @ w_in, A_WIDTHS)
    q = rope(q.reshape(b, t, N_HEADS_A, HEAD_DIM), pos)
    k = rope(k.reshape(b, t, N_KV_A, HEAD_DIM), pos)
    v = v.reshape(b, t, N_KV_A, HEAD_DIM)
    qi = partial_rope(qi.reshape(b, t, N_IDX_HEADS, IDX_DIM), pos)
    ki = partial_rope(ki.reshape(b, t, 1, IDX_DIM), pos)[:, :, 0]
    wi = wi * (N_IDX_HEADS ** -0.5 * IDX_DIM ** -0.5)
    if past is None:
        k_all, v_all, ki_all = k, v, ki
    else:
        pk, pv, pki = past
        k_all = jnp.concatenate([pk.astype(k.dtype), k], axis=1)
        v_all = jnp.concatenate([pv.astype(v.dtype), v], axis=1)
        ki_all = jnp.concatenate([pki.astype(ki.dtype), ki], axis=1)
    n_keys = k_all.shape[1]
    topk = min(TOPK_CAP, n_keys // 4)
    key_chunk = jnp.arange(n_keys, dtype=jnp.int32) // CHUNK

    def block(qp, qb, qib, wib):
        nq = qp.shape[0]
        rel = jax.nn.relu(jnp.einsum('bqhd,bsd->bqhs', qib, ki_all).astype(jnp.float32))
        score = jnp.einsum('bqhs,bqh->bqs', rel, wib.astype(jnp.float32))
        q_chunk = qp // CHUNK
        adm = key_chunk[None, :] <= q_chunk[:, None]
        score = jnp.where(adm[None], score, -jnp.inf)
        _, idx = lax.top_k(score, topk)
        k_sel = jax.vmap(lambda kb, ib: kb[ib])(k_all, idx)
        v_sel = jax.vmap(lambda vb, ib: vb[ib])(v_all, idx)
        valid = key_chunk[idx] <= q_chunk[None, :, None]
        qg = qb.reshape(b, nq, N_KV_A, GQA_REP, HEAD_DIM)
        logit = jnp.einsum('bqgrd,bqkgd->bqgrk', qg, k_sel).astype(jnp.float32) * (HEAD_DIM ** -0.5)
        logit = jnp.where(valid[:, :, None, None, :], logit, -jnp.inf)
        p = jax.nn.softmax(logit, axis=-1).astype(v_sel.dtype)
        o = jnp.einsum('bqgrk,bqkgd->bqgrd', p, v_sel)
        return o.reshape(b, nq, N_HEADS_A * HEAD_DIM)

    o = sweep_queries(block, pos, q, qi, wi)
    return o @ w_out, (k, v, ki)


def fox_mixer(h, pos, w_in, b_forget, w_out, past):
    b, t, _ = h.shape
    q, k, v, g, f = split_cols(h @ w_in, B_WIDTHS)
    q = q.reshape(b, t, N_HEADS_B, HEAD_DIM)
    k = k.reshape(b, t, N_HEADS_B, HEAD_DIM)
    v = v.reshape(b, t, N_HEADS_B, HEAD_DIM)
    logf = jax.nn.log_sigmoid(f.astype(jnp.float32) + b_forget.astype(jnp.float32))
    if past is None:
        k_all, v_all, lf_all = k, v, logf
    else:
        pk, pv, plf = past
        k_all = jnp.concatenate([pk.astype(k.dtype), k], axis=1)
        v_all = jnp.concatenate([pv.astype(v.dtype), v], axis=1)
        lf_all = jnp.concatenate([plf.astype(jnp.float32), logf], axis=1)
    n_keys = k_all.shape[1]
    cum = jnp.cumsum(lf_all, axis=1)
    cum_keys = jnp.transpose(cum, (0, 2, 1))
    cum_q = cum[:, n_keys - t:]
    key_pos = jnp.arange(n_keys, dtype=jnp.int32)

    def block(qp, qb, cqb):
        nq = qp.shape[0]
        logit = jnp.einsum('bqhd,bshd->bhqs', qb, k_all).astype(jnp.float32) * (HEAD_DIM ** -0.5)
        logit = logit + (jnp.transpose(cqb, (0, 2, 1))[:, :, :, None] - cum_keys[:, :, None, :])
        causal = key_pos[None, :] <= qp[:, None]
        logit = jnp.where(causal[None, None], logit, -jnp.inf)
        p = jax.nn.softmax(logit, axis=-1).astype(v_all.dtype)
        o = jnp.einsum('bhqs,bshd->bqhd', p, v_all)
        return o.reshape(b, nq, B_INNER)

    o = sweep_queries(block, pos, q, cum_q)
    o = o * jax.nn.sigmoid(g)
    return o @ w_out, (k, v, logf)


def swiglu(h, w_in, w_out):
    gate, up = split_cols(h @ w_in, (D_FF, D_FF))
    return (jax.nn.silu(gate) * up) @ w_out


def run_trunk(x, c, pos, caches, w_ada, b_ada, g_mix, g_ffn, w_in_a, w_out_a, w_in_b, b_forget, w_out_b,
              w_ffn_in, w_ffn_out, w_ada_final, b_ada_final, g_final):
    a_k, a_v, a_ki, b_k, b_v, b_lf = [], [], [], [], [], []
    cs = jax.nn.silu(c)
    for i in range(DEPTH):
        sh1, sc1, gt1, sh2, sc2, gt2 = jnp.split(cs @ w_ada[i] + b_ada[i], 6, axis=-1)
        h = ada_modulate(rms_norm(x, g_mix[i]), sh1, sc1)
        j = i // N_MIXERS
        if i % N_MIXERS == 0:
            past = None if caches is None else (caches[0][j], caches[1][j], caches[2][j])
            o, (k, v, ki) = dsa_mixer(h, pos, w_in_a[j], w_out_a[j], past)
            a_k.append(k); a_v.append(v); a_ki.append(ki)
        else:
            past = None if caches is None else (caches[3][j], caches[4][j], caches[5][j])
            o, (k, v, lf) = fox_mixer(h, pos, w_in_b[j], b_forget[j], w_out_b[j], past)
            b_k.append(k); b_v.append(v); b_lf.append(lf)
        x = x + gt1[:, None, :] * o
        h = ada_modulate(rms_norm(x, g_ffn[i]), sh2, sc2)
        x = x + gt2[:, None, :] * swiglu(h, w_ffn_in[i], w_ffn_out[i])
    shf, scf = jnp.split(cs @ w_ada_final + b_ada_final, 2, axis=-1)
    y = ada_modulate(rms_norm(x, g_final), shf, scf)
    return y, (jnp.stack(a_k), jnp.stack(a_v), jnp.stack(a_ki), jnp.stack(b_k), jnp.stack(b_v), jnp.stack(b_lf))


def setup_inputs(seed: int = 0) -> dict:
    key = jax.random.key(seed)
    ks = jax.random.split(key, 24)

    def nrm(k, shape, s):
        return jax.random.normal(k, shape, jnp.float32) * s

    d = D_MODEL
    return {
        "x_prompt": nrm(ks[0], (BATCH, SEQ, d), 1.0),
        "x_sample": nrm(ks[1], (DEC_BATCH, DEC_SEQ, d), 1.0),
        "cache_a_k": nrm(ks[2], (N_LAYERS_A, DEC_BATCH, PAST_LEN, N_KV_A, HEAD_DIM), 1.0),
        "cache_a_v": nrm(ks[3], (N_LAYERS_A, DEC_BATCH, PAST_LEN, N_KV_A, HEAD_DIM), 1.0),
        "cache_a_kidx": nrm(ks[4], (N_LAYERS_A, DEC_BATCH, PAST_LEN, IDX_DIM), 1.0),
        "cache_b_k": nrm(ks[5], (N_LAYERS_B, DEC_BATCH, PAST_LEN, N_HEADS_B, HEAD_DIM), 1.0),
        "cache_b_v": nrm(ks[6], (N_LAYERS_B, DEC_BATCH, PAST_LEN, N_HEADS_B, HEAD_DIM), 1.0),
        "cache_b_logf": jax.nn.log_sigmoid(FORGET_BIAS_MEAN + nrm(ks[7], (N_LAYERS_B, DEC_BATCH, PAST_LEN, N_HEADS_B), 1.0)),
        "c_prompt": nrm(ks[8], (BATCH, d), 1.0),
        "c_sample": nrm(ks[9], (DEC_BATCH, d), 1.0),
        "w_ada": nrm(ks[10], (DEPTH, d, 6 * d), 0.5 * d ** -0.5),
        "b_ada": nrm(ks[11], (DEPTH, 6 * d), 0.01),
        "g_mix": 1.0 + nrm(ks[12], (DEPTH, d), 0.02),
        "g_ffn": 1.0 + nrm(ks[13], (DEPTH, d), 0.02),
        "w_in_a": nrm(ks[14], (N_LAYERS_A, d, A_IN), d ** -0.5),
        "w_out_a": nrm(ks[15], (N_LAYERS_A, N_HEADS_A * HEAD_DIM, d), (N_HEADS_A * HEAD_DIM) ** -0.5),
        "w_in_b": nrm(ks[16], (N_LAYERS_B, d, B_IN), d ** -0.5),
        "b_forget": FORGET_BIAS_MEAN + nrm(ks[17], (N_LAYERS_B, N_HEADS_B), 0.1),
        "w_out_b": nrm(ks[18], (N_LAYERS_B, B_INNER, d), B_INNER ** -0.5),
        "w_ffn_in": nrm(ks[19], (DEPTH, d, 2 * D_FF), d ** -0.5),
        "w_ffn_out": nrm(ks[20], (DEPTH, D_FF, d), D_FF ** -0.5),
        "w_ada_final": nrm(ks[21], (d, 2 * d), 0.5 * d ** -0.5),
        "b_ada_final": nrm(ks[22], (2 * d,), 0.01),
        "g_final": 1.0 + nrm(ks[23], (d,), 0.02),
    }


def reference(x_prompt, x_sample, cache_a_k, cache_a_v, cache_a_kidx, cache_b_k, cache_b_v, cache_b_logf,
              c_prompt, c_sample, w_ada, b_ada, g_mix, g_ffn, w_in_a, w_out_a, w_in_b, b_forget, w_out_b,
              w_ffn_in, w_ffn_out, w_ada_final, b_ada_final, g_final):
    pos_p = jnp.arange(x_prompt.shape[1], dtype=jnp.int32)
    pos_s = cache_a_k.shape[2] + jnp.arange(x_sample.shape[1], dtype=jnp.int32)
    y_prompt, (pak, pav, paki, pbk, pbv, pblf) = run_trunk(
        x_prompt, c_prompt, pos_p, None, w_ada, b_ada, g_mix, g_ffn, w_in_a, w_out_a, w_in_b, b_forget,
        w_out_b, w_ffn_in, w_ffn_out, w_ada_final, b_ada_final, g_final)
    y_sample, (sak, sav, saki, sbk, sbv, sblf) = run_trunk(
        x_sample, c_sample, pos_s, (cache_a_k, cache_a_v, cache_a_kidx, cache_b_k, cache_b_v, cache_b_logf),
        w_ada, b_ada, g_mix, g_ffn, w_in_a, w_out_a, w_in_b, b_forget, w_out_b, w_ffn_in, w_ffn_out,
        w_ada_final, b_ada_final, g_final)
    return (y_prompt, y_sample, pak, pav, paki, pbk, pbv, pblf, sak, sav, saki, sbk, sbv, sblf)
```

```python
import functools

import jax
import jax.numpy as jnp
from jax import lax
from jax.experimental import pallas as pl
from jax.experimental.pallas import tpu as pltpu

CHUNK = 64
HEAD_DIM = 64
N_HEADS_A = 16
N_KV_A = 4
GQA_REP = N_HEADS_A // N_KV_A
N_IDX_HEADS = 8
IDX_DIM = 64
IDX_ROPE_DIM = 32
TOPK_CAP = 256
N_HEADS_B = 16
ROPE_THETA = 10000.0
EPS = 1e-6
WI_SCALE = N_IDX_HEADS ** -0.5 * IDX_DIM ** -0.5
Q_SCALE = HEAD_DIM ** -0.5

LANES = 128
_MXU = jnp.bfloat16
_NEG = -1e30
_INT_MIN = -(2 ** 31)
_VMEM_LIMIT = 50 * 1024 * 1024


def _dot(a, b):
    return jnp.dot(a, b, preferred_element_type=jnp.float32)


def _dot_nt(a, b):
    return lax.dot_general(a, b, (((1,), (1,)), ((), ())), preferred_element_type=jnp.float32)


def _params():
    return pltpu.CompilerParams(vmem_limit_bytes=_VMEM_LIMIT)


def _const_spec(shape):
    nd = len(shape)
    return pl.BlockSpec(shape, lambda *_: (0,) * nd)


def _ada_kernel(c_ref, w_ref, b_ref, o_ref):
    c = c_ref[...]
    cs = c * jax.nn.sigmoid(c)
    o_ref[...] = jnp.dot(cs, w_ref[...], preferred_element_type=jnp.float32,
                         precision=lax.Precision.HIGHEST) + b_ref[...]


def _ada(c, w, b, tn):
    nl, d, n = w.shape
    r = c.shape[0]
    return pl.pallas_call(
        _ada_kernel,
        grid=(nl, n // tn),
        in_specs=[
            pl.BlockSpec((r, d), lambda l, j: (0, 0)),
            pl.BlockSpec((None, d, tn), lambda l, j: (l, 0, j)),
            pl.BlockSpec((None, 1, tn), lambda l, j: (l, 0, j)),
        ],
        out_specs=pl.BlockSpec((None, r, tn), lambda l, j: (l, 0, j)),
        out_shape=jax.ShapeDtypeStruct((nl, r, n), jnp.float32),
        compiler_params=_params(),
        name="ada",
    )(c, w, b)


def _norm_mod(x, g, shift, scale):
    ms = jnp.mean(x * x, axis=-1, keepdims=True)
    y = x * lax.rsqrt(ms + EPS) * g
    return y * (1.0 + scale) + shift


def _rope128(xc, cos, sin, first, sh):
    partner = jnp.where(first, pltpu.roll(xc, LANES - sh, 1), pltpu.roll(xc, sh, 1))
    return xc * cos + partner * sin


def _row_specs(nb, tm, width):
    return pl.BlockSpec((nb, tm, width), lambda b, t: (b, t, 0))


def _head_spec(nb, nh, tm):
    return pl.BlockSpec((nb, nh, tm, HEAD_DIM), lambda b, t: (b, 0, t, 0))


def _tab_spec(nb, tm):
    if nb == 1:
        return pl.BlockSpec((tm, LANES), lambda b, t: (t, 0))
    return pl.BlockSpec((nb * tm, LANES), lambda b, t: (0, 0))


def _a_in_kernel(x_ref, m_ref, g_ref, wq_ref, wk_ref, wv_ref, wqi_ref, wkw_ref,
                 cq_ref, sq_ref, ci_ref, si_ref,
                 qh_ref, kl_ref, vl_ref, kil_ref, kh_ref, vh_ref, kib_ref, qih_ref, wi_ref):
    nb, tm, d = x_ref.shape
    m = nb * tm
    h = _norm_mod(x_ref[...], g_ref[...], m_ref[:, 0:1, :], m_ref[:, 1:2, :])
    hb = h.reshape(m, d).astype(_MXU)
    lane = lax.broadcasted_iota(jnp.int32, (m, LANES), 1)
    first_q = (lane % HEAD_DIM) < (HEAD_DIM // 2)
    first_i = (lane % IDX_ROPE_DIM) < (IDX_ROPE_DIM // 2)
    cq, sq, ci, si = cq_ref[...], sq_ref[...], ci_ref[...], si_ref[...]

    q = _dot(hb, wq_ref[...])
    for j in range(N_HEADS_A // 2):
        c = _rope128(q[:, j * LANES:(j + 1) * LANES], cq, sq, first_q, HEAD_DIM // 2) * Q_SCALE
        qh_ref[:, 2 * j] = c[:, :HEAD_DIM].reshape(nb, tm, HEAD_DIM).astype(qh_ref.dtype)
        qh_ref[:, 2 * j + 1] = c[:, HEAD_DIM:].reshape(nb, tm, HEAD_DIM).astype(qh_ref.dtype)

    k = _dot(hb, wk_ref[...])
    for j in range(N_KV_A // 2):
        c = _rope128(k[:, j * LANES:(j + 1) * LANES], cq, sq, first_q, HEAD_DIM // 2)
        kl_ref[:, :, j * LANES:(j + 1) * LANES] = c.reshape(nb, tm, LANES)
        kh_ref[:, 2 * j] = c[:, :HEAD_DIM].reshape(nb, tm, HEAD_DIM).astype(kh_ref.dtype)
        kh_ref[:, 2 * j + 1] = c[:, HEAD_DIM:].reshape(nb, tm, HEAD_DIM).astype(kh_ref.dtype)

    v = _dot(hb, wv_ref[...])
    vl_ref[...] = v.reshape(nb, tm, N_KV_A * HEAD_DIM)
    for g in range(N_KV_A):
        vh_ref[:, g] = v[:, g * HEAD_DIM:(g + 1) * HEAD_DIM].reshape(nb, tm, HEAD_DIM).astype(vh_ref.dtype)

    qi = _dot(hb, wqi_ref[...])
    for j in range(N_IDX_HEADS // 2):
        c = _rope128(qi[:, j * LANES:(j + 1) * LANES], ci, si, first_i, IDX_ROPE_DIM // 2)
        qih_ref[:, 2 * j] = c[:, :IDX_DIM].reshape(nb, tm, IDX_DIM).astype(qih_ref.dtype)
        qih_ref[:, 2 * j + 1] = c[:, IDX_DIM:].reshape(nb, tm, IDX_DIM).astype(qih_ref.dtype)

    kw = _dot(hb, wkw_ref[...])
    is_ki = lane < IDX_DIM
    kw = _rope128(kw, jnp.where(is_ki, ci, 1.0), jnp.where(is_ki, si, 0.0), first_i, IDX_ROPE_DIM // 2)
    ki = kw[:, :IDX_DIM].reshape(nb, tm, IDX_DIM)
    kil_ref[...] = ki
    kib_ref[...] = ki.astype(kib_ref.dtype)
    wi_ref[...] = (kw[:, IDX_DIM:IDX_DIM + N_IDX_HEADS] * WI_SCALE).reshape(nb, tm, N_IDX_HEADS)


def _a_in(x, mods, g, wts, tabs, nb, tm):
    b, t, d = x.shape
    wq, wk, wv, wqi, wkw = wts
    f32, bf = jnp.float32, _MXU
    kvw = N_KV_A * HEAD_DIM
    out_shape = (
        jax.ShapeDtypeStruct((b, N_HEADS_A, t, HEAD_DIM), bf),
        jax.ShapeDtypeStruct((b, t, kvw), f32),
        jax.ShapeDtypeStruct((b, t, kvw), f32),
        jax.ShapeDtypeStruct((b, t, IDX_DIM), f32),
        jax.ShapeDtypeStruct((b, N_KV_A, t, HEAD_DIM), bf),
        jax.ShapeDtypeStruct((b, N_KV_A, t, HEAD_DIM), bf),
        jax.ShapeDtypeStruct((b, t, IDX_DIM), bf),
        jax.ShapeDtypeStruct((b, N_IDX_HEADS, t, IDX_DIM), bf),
        jax.ShapeDtypeStruct((b, t, N_IDX_HEADS), f32),
    )
    out_specs = (
        _head_spec(nb, N_HEADS_A, tm), _row_specs(nb, tm, kvw), _row_specs(nb, tm, kvw),
        _row_specs(nb, tm, IDX_DIM), _head_spec(nb, N_KV_A, tm), _head_spec(nb, N_KV_A, tm),
        _row_specs(nb, tm, IDX_DIM), _head_spec(nb, N_IDX_HEADS, tm), _row_specs(nb, tm, N_IDX_HEADS),
    )
    in_specs = [
        _row_specs(nb, tm, d),
        pl.BlockSpec((nb, 6, d), lambda bi, ti: (bi, 0, 0)),
        _const_spec(g.shape),
        _const_spec(wq.shape), _const_spec(wk.shape), _const_spec(wv.shape),
        _const_spec(wqi.shape), _const_spec(wkw.shape),
        _tab_spec(nb, tm), _tab_spec(nb, tm), _tab_spec(nb, tm), _tab_spec(nb, tm),
    ]
    return pl.pallas_call(
        _a_in_kernel, grid=(b // nb, t // tm), in_specs=in_specs, out_specs=out_specs,
        out_shape=out_shape, compiler_params=_params(), name="a_in",
    )(x, mods, g, wq, wk, wv, wqi, wkw, *tabs)


def _sort_key(s):
    bits = pltpu.bitcast(s, jnp.int32)
    key = bits ^ ((bits >> 31) & 0x7FFFFFFF)
    return jnp.where(s == 0.0, 0, key)


def _dsa_kernel(qh_ref, kh_ref, vh_ref, qih_ref, ki_ref, wi_ref, o_ref,
                sk_ref, m_ref, l_ref, acc_ref, *, tq, tk, q_off, topk, idx_bits):
    q0 = q_off + pl.program_id(1) * tq
    row = lax.broadcasted_iota(jnp.int32, (tq, 1), 0)
    n_adm = ((q0 + row) // CHUNK + 1) * CHUNK
    n_adm_max = ((q0 + tq - 1) // CHUNK + 1) * CHUNK
    nkb = (n_adm_max + tk - 1) // tk
    kf = jnp.float32(topk)

    def col_of(ks):
        return ks + lax.broadcasted_iota(jnp.int32, (tq, tk), 1)

    qi = qih_ref[0].reshape(N_IDX_HEADS * tq, IDX_DIM)
    wi = wi_ref[0]

    def score_body(kb, carry):
        ks = pl.multiple_of(kb * tk, tk)
        r = jnp.maximum(_dot_nt(qi, ki_ref[0, pl.ds(ks, tk), :]), 0.0)
        s = r[0:tq] * wi[:, 0:1]
        for h in range(1, N_IDX_HEADS):
            s = s + r[h * tq:(h + 1) * tq] * wi[:, h:h + 1]
        sk_ref[:, pl.ds(ks, tk)] = jnp.where(col_of(ks) < n_adm, _sort_key(s), _INT_MIN)
        return carry

    lax.fori_loop(0, nkb, score_body, 0)

    def count(pred):
        def body(kb, acc):
            ks = pl.multiple_of(kb * tk, tk)
            hit = jnp.where(pred(sk_ref[:, pl.ds(ks, tk)], ks), 1.0, 0.0)
            part = hit[:, 0:LANES]
            for c in range(1, tk // LANES):
                part = part + hit[:, c * LANES:(c + 1) * LANES]
            return acc + part
        acc = lax.fori_loop(0, nkb, body, jnp.zeros((tq, LANES), jnp.float32))
        return jnp.sum(acc, axis=1, keepdims=True)

    c0 = count(lambda blk, ks: blk >= 0)
    thr = jnp.where(c0 >= kf, 0, _INT_MIN).astype(jnp.int32)

    def bit_body(i, thr):
        cand = thr | lax.shift_left(jnp.int32(1), 30 - i)
        c = count(lambda blk, ks: blk >= cand)
        return jnp.where(c >= kf, cand, thr)

    thr = lax.fori_loop(0, 31, bit_body, thr)
    thr = jnp.maximum(thr, _INT_MIN + 1)
    c_ge = count(lambda blk, ks: blk >= thr)

    @pl.when(jnp.max(c_ge) > kf)
    def _():
        c_gt = count(lambda blk, ks: blk > thr)
        need = jnp.where(c_ge > kf, kf - c_gt, jnp.float32(3e38))

        def idx_body(i, x):
            cand = x | lax.shift_left(jnp.int32(1), idx_bits - 1 - i)
            c = count(lambda blk, ks: (blk == thr) & (col_of(ks) < cand))
            return jnp.where(c < need, cand, x)

        last = lax.fori_loop(0, idx_bits, idx_body, jnp.zeros((tq, 1), jnp.int32))

        def drop_body(kb, carry):
            ks = pl.multiple_of(kb * tk, tk)
            blk = sk_ref[:, pl.ds(ks, tk)]
            sk_ref[:, pl.ds(ks, tk)] = jnp.where((blk == thr) & (col_of(ks) > last), _INT_MIN, blk)
            return carry

        lax.fori_loop(0, nkb, drop_body, 0)

    m_ref[...] = jnp.full(m_ref.shape, _NEG, jnp.float32)
    l_ref[...] = jnp.zeros(l_ref.shape, jnp.float32)
    acc_ref[...] = jnp.zeros(acc_ref.shape, jnp.float32)

    def attn_body(kb, carry):
        ks = pl.multiple_of(kb * tk, tk)
        bias = jnp.where(sk_ref[:, pl.ds(ks, tk)] >= thr, 0.0, _NEG)
        for g in range(N_KV_A):
            q = qh_ref[0, g * GQA_REP:(g + 1) * GQA_REP].reshape(GQA_REP * tq, HEAD_DIM)
            s = _dot_nt(q, kh_ref[0, g, pl.ds(ks, tk), :])
            s = (s.reshape(GQA_REP, tq, tk) + bias[None]).reshape(GQA_REP * tq, tk)
            m_old = m_ref[g]
            m_new = jnp.maximum(m_old, jnp.max(s, axis=1, keepdims=True))
            alpha = jnp.exp(m_old - m_new)
            p = jnp.exp(s - m_new)
            l_ref[g] = alpha * l_ref[g] + jnp.sum(p, axis=1, keepdims=True)
            acc_ref[g] = alpha * acc_ref[g] + _dot(p.astype(_MXU), vh_ref[0, g, pl.ds(ks, tk), :])
            m_ref[g] = m_new
        return carry

    lax.fori_loop(0, nkb, attn_body, 0)

    for g in range(N_KV_A):
        o = acc_ref[g] / l_ref[g]
        for r in range(0, GQA_REP, 2):
            pair = jnp.concatenate([o[r * tq:(r + 1) * tq], o[(r + 1) * tq:(r + 2) * tq]], axis=1)
            c0_ = (g * GQA_REP + r) * HEAD_DIM
            o_ref[0, :, c0_:c0_ + LANES] = pair.astype(o_ref.dtype)


def _dsa(qh, kh, vh, qih, ki, wi, tq, tk, q_off, topk):
    b, _, t, _ = qh.shape
    lk = kh.shape[2]
    kern = functools.partial(_dsa_kernel, tq=tq, tk=tk, q_off=q_off, topk=topk,
                             idx_bits=max(1, (lk - 1).bit_length()))
    return pl.pallas_call(
        kern, grid=(b, t // tq),
        in_specs=[
            pl.BlockSpec((1, N_HEADS_A, tq, HEAD_DIM), lambda bi, qi: (bi, 0, qi, 0)),
            pl.BlockSpec((1, N_KV_A, lk, HEAD_DIM), lambda bi, qi: (bi, 0, 0, 0)),
            pl.BlockSpec((1, N_KV_A, lk, HEAD_DIM), lambda bi, qi: (bi, 0, 0, 0)),
            pl.BlockSpec((1, N_IDX_HEADS, tq, IDX_DIM), lambda bi, qi: (bi, 0, qi, 0)),
            pl.BlockSpec((1, lk, IDX_DIM), lambda bi, qi: (bi, 0, 0)),
            pl.BlockSpec((1, tq, N_IDX_HEADS), lambda bi, qi: (bi, qi, 0)),
        ],
        out_specs=pl.BlockSpec((1, tq, N_HEADS_A * HEAD_DIM), lambda bi, qi: (bi, qi, 0)),
        out_shape=jax.ShapeDtypeStruct((b, t, N_HEADS_A * HEAD_DIM), _MXU),
        scratch_shapes=[
            pltpu.VMEM((tq, lk), jnp.int32),
            pltpu.VMEM((N_KV_A, GQA_REP * tq, 1), jnp.float32),
            pltpu.VMEM((N_KV_A, GQA_REP * tq, 1), jnp.float32),
            pltpu.VMEM((N_KV_A, GQA_REP * tq, HEAD_DIM), jnp.float32),
        ],
        compiler_params=_params(), name="dsa_attn",
    )(qh, kh, vh, qih, ki, wi)


def _b_in_kernel(x_ref, m_ref, g_ref, wq_ref, wk_ref, wv_ref, wg_ref, wf_ref, bf_ref,
                 qh_ref, kl_ref, vl_ref, kh_ref, vh_ref, gate_ref, lf_ref):
    nb, tm, d = x_ref.shape
    m = nb * tm
    h = _norm_mod(x_ref[...], g_ref[...], m_ref[:, 0:1, :], m_ref[:, 1:2, :])
    hb = h.reshape(m, d).astype(_MXU)

    q = _dot(hb, wq_ref[...]) * Q_SCALE
    for hd in range(N_HEADS_B):
        qh_ref[:, hd] = q[:, hd * HEAD_DIM:(hd + 1) * HEAD_DIM].reshape(nb, tm, HEAD_DIM).astype(qh_ref.dtype)
    k = _dot(hb, wk_ref[...])
    kl_ref[...] = k.reshape(nb, tm, N_HEADS_B * HEAD_DIM)
    for hd in range(N_HEADS_B):
        kh_ref[:, hd] = k[:, hd * HEAD_DIM:(hd + 1) * HEAD_DIM].reshape(nb, tm, HEAD_DIM).astype(kh_ref.dtype)
    v = _dot(hb, wv_ref[...])
    vl_ref[...] = v.reshape(nb, tm, N_HEADS_B * HEAD_DIM)
    for hd in range(N_HEADS_B):
        vh_ref[:, hd] = v[:, hd * HEAD_DIM:(hd + 1) * HEAD_DIM].reshape(nb, tm, HEAD_DIM).astype(vh_ref.dtype)
    gate_ref[...] = _dot(hb, wg_ref[...]).reshape(nb, tm, N_HEADS_B * HEAD_DIM)
    f = _dot(hb, wf_ref[...])[:, :N_HEADS_B] + bf_ref[...]
    lf = jnp.minimum(f, 0.0) - jnp.log1p(jnp.exp(-jnp.abs(f)))
    lf_ref[...] = lf.reshape(nb, tm, N_HEADS_B)


def _b_in(x, mods, g, wts, bf, nb, tm):
    b, t, d = x.shape
    wq, wk, wv, wg, wf = wts
    f32, bft = jnp.float32, _MXU
    inner = N_HEADS_B * HEAD_DIM
    out_shape = (
        jax.ShapeDtypeStruct((b, N_HEADS_B, t, HEAD_DIM), bft),
        jax.ShapeDtypeStruct((b, t, inner), f32),
        jax.ShapeDtypeStruct((b, t, inner), f32),
        jax.ShapeDtypeStruct((b, N_HEADS_B, t, HEAD_DIM), bft),
        jax.ShapeDtypeStruct((b, N_HEADS_B, t, HEAD_DIM), bft),
        jax.ShapeDtypeStruct((b, t, inner), f32),
        jax.ShapeDtypeStruct((b, t, N_HEADS_B), f32),
    )
    out_specs = (
        _head_spec(nb, N_HEADS_B, tm), _row_specs(nb, tm, inner), _row_specs(nb, tm, inner),
        _head_spec(nb, N_HEADS_B, tm), _head_spec(nb, N_HEADS_B, tm), _row_specs(nb, tm, inner),
        _row_specs(nb, tm, N_HEADS_B),
    )
    in_specs = [
        _row_specs(nb, tm, d),
        pl.BlockSpec((nb, 6, d), lambda bi, ti: (bi, 0, 0)),
        _const_spec(g.shape),
        _const_spec(wq.shape), _const_spec(wk.shape), _const_spec(wv.shape),
        _const_spec(wg.shape), _const_spec(wf.shape), _const_spec(bf.shape),
    ]
    return pl.pallas_call(
        _b_in_kernel, grid=(b // nb, t // tm), in_specs=in_specs, out_specs=out_specs,
        out_shape=out_shape, compiler_params=_params(), name="b_in",
    )(x, mods, g, wq, wk, wv, wg, wf, bf)


_CUM_BLK = 256


def _cumsum_kernel(x_ref, o_ref):
    n = x_ref.shape[2]
    r = lax.broadcasted_iota(jnp.int32, (_CUM_BLK, _CUM_BLK), 0)
    c = lax.broadcasted_iota(jnp.int32, (_CUM_BLK, _CUM_BLK), 1)
    tri = jnp.where(r <= c, 1.0, 0.0)
    carry = jnp.zeros((x_ref.shape[1], 1), jnp.float32)
    for j in range(n // _CUM_BLK):
        xs = x_ref[0, :, j * _CUM_BLK:(j + 1) * _CUM_BLK]
        cs = jnp.dot(xs, tri, preferred_element_type=jnp.float32, precision=lax.Precision.HIGHEST) + carry
        o_ref[0, :, j * _CUM_BLK:(j + 1) * _CUM_BLK] = cs
        carry = cs[:, _CUM_BLK - 1:_CUM_BLK]


def _cumsum(x):
    b, h, n = x.shape
    return pl.pallas_call(
        _cumsum_kernel, grid=(b,),
        in_specs=[pl.BlockSpec((1, h, n), lambda bi: (bi, 0, 0))],
        out_specs=pl.BlockSpec((1, h, n), lambda bi: (bi, 0, 0)),
        out_shape=jax.ShapeDtypeStruct((b, h, n), jnp.float32),
        compiler_params=_params(), name="cumsum",
    )(x)


_FOX_HPB = 2


def _fox_kernel(qh_ref, kh_ref, vh_ref, g_ref, cq_ref, ck_ref, o_ref, *, tq, tk, q_off):
    q0 = q_off + pl.program_id(2) * tq
    nkb = (q0 + tq + tk - 1) // tk
    rowpos = q0 + lax.broadcasted_iota(jnp.int32, (tq, 1), 0)
    outs = []
    for r in range(_FOX_HPB):
        q = qh_ref[0, r]
        cq = cq_ref[0, r]

        def body(kb, carry, r=r, q=q, cq=cq):
            m_old, l_old, acc = carry
            ks = pl.multiple_of(kb * tk, tk)
            s = _dot_nt(q, kh_ref[0, r, pl.ds(ks, tk), :]) + (cq - ck_ref[0, r, :, pl.ds(ks, tk)])
            col = ks + lax.broadcasted_iota(jnp.int32, (tq, tk), 1)
            s = jnp.where(col <= rowpos, s, _NEG)
            m_new = jnp.maximum(m_old, jnp.max(s, axis=1, keepdims=True))
            alpha = jnp.exp(m_old - m_new)
            p = jnp.exp(s - m_new)
            l_new = alpha * l_old + jnp.sum(p, axis=1, keepdims=True)
            acc = alpha * acc + _dot(p.astype(_MXU), vh_ref[0, r, pl.ds(ks, tk), :])
            return m_new, l_new, acc

        init = (jnp.full((tq, 1), _NEG, jnp.float32), jnp.zeros((tq, 1), jnp.float32),
                jnp.zeros((tq, HEAD_DIM), jnp.float32))
        _, l_fin, acc = lax.fori_loop(0, nkb, body, init)
        outs.append(acc / l_fin)
    o = jnp.concatenate(outs, axis=1)
    o_ref[0] = (o * jax.nn.sigmoid(g_ref[0])).astype(o_ref.dtype)


def _fox(qh, kh, vh, gate, cq, ck, tq, tk, q_off):
    b, nh, t, _ = qh.shape
    lk = kh.shape[2]
    hp = _FOX_HPB
    kern = functools.partial(_fox_kernel, tq=tq, tk=tk, q_off=q_off)
    return pl.pallas_call(
        kern, grid=(b, nh // hp, t // tq),
        in_specs=[
            pl.BlockSpec((1, hp, tq, HEAD_DIM), lambda bi, hi, qi: (bi, hi, qi, 0)),
            pl.BlockSpec((1, hp, lk, HEAD_DIM), lambda bi, hi, qi: (bi, hi, 0, 0)),
            pl.BlockSpec((1, hp, lk, HEAD_DIM), lambda bi, hi, qi: (bi, hi, 0, 0)),
            pl.BlockSpec((1, tq, hp * HEAD_DIM), lambda bi, hi, qi: (bi, qi, hi)),
            pl.BlockSpec((1, hp, tq, 1), lambda bi, hi, qi: (bi, hi, qi, 0)),
            pl.BlockSpec((1, hp, 1, lk), lambda bi, hi, qi: (bi, hi, 0, 0)),
        ],
        out_specs=pl.BlockSpec((1, tq, hp * HEAD_DIM), lambda bi, hi, qi: (bi, qi, hi)),
        out_shape=jax.ShapeDtypeStruct((b, t, nh * HEAD_DIM), _MXU),
        compiler_params=_params(), name="fox_attn",
    )(qh, kh, vh, gate, cq, ck)


def _post_kernel(x_ref, o_ref, m_ref, g_ref, wo_ref, w1_ref, w2_ref, y_ref, *, n_ff_chunks):
    nb, tm, d = x_ref.shape
    m = nb * tm
    d_ff = w2_ref.shape[0]
    cw = d_ff // n_ff_chunks
    att = _dot(o_ref[...].reshape(m, o_ref.shape[2]), wo_ref[...]).reshape(nb, tm, d)
    x1 = x_ref[...] + m_ref[:, 2:3, :] * att
    h = _norm_mod(x1, g_ref[...], m_ref[:, 3:4, :], m_ref[:, 4:5, :]).reshape(m, d).astype(_MXU)
    acc = None
    for j in range(n_ff_chunks):
        gate = _dot(h, w1_ref[:, j * cw:(j + 1) * cw])
        up = _dot(h, w1_ref[:, d_ff + j * cw:d_ff + (j + 1) * cw])
        act = (gate * jax.nn.sigmoid(gate) * up).astype(_MXU)
        part = _dot(act, w2_ref[j * cw:(j + 1) * cw, :])
        acc = part if acc is None else acc + part
    y_ref[...] = x1 + m_ref[:, 5:6, :] * acc.reshape(nb, tm, d)


def _post(x, o, mods, g, wo, w1, w2, nb, tm):
    b, t, d = x.shape
    d_ff = w2.shape[0]
    n_chunks = 2 if (d_ff // 2) % LANES == 0 else 1
    single = pl.Buffered(1)
    return pl.pallas_call(
        functools.partial(_post_kernel, n_ff_chunks=n_chunks),
        grid=(b // nb, t // tm),
        in_specs=[
            _row_specs(nb, tm, d),
            _row_specs(nb, tm, o.shape[2]),
            pl.BlockSpec((nb, 6, d), lambda bi, ti: (bi, 0, 0)),
            _const_spec(g.shape),
            pl.BlockSpec(wo.shape, lambda bi, ti: (0, 0), pipeline_mode=single),
            pl.BlockSpec(w1.shape, lambda bi, ti: (0, 0), pipeline_mode=single),
            pl.BlockSpec(w2.shape, lambda bi, ti: (0, 0), pipeline_mode=single),
        ],
        out_specs=_row_specs(nb, tm, d),
        out_shape=jax.ShapeDtypeStruct((b, t, d), jnp.float32),
        compiler_params=_params(), name="post",
    )(x, o, mods, g, wo, w1, w2)


def _final_kernel(x_ref, m_ref, g_ref, y_ref):
    y_ref[...] = _norm_mod(x_ref[...], g_ref[...], m_ref[:, 0:1, :], m_ref[:, 1:2, :])


def _final(x, mods, g, nb, tm):
    b, t, d = x.shape
    return pl.pallas_call(
        _final_kernel, grid=(b // nb, t // tm),
        in_specs=[_row_specs(nb, tm, d), pl.BlockSpec((nb, 2, d), lambda bi, ti: (bi, 0, 0)),
                  _const_spec(g.shape)],
        out_specs=_row_specs(nb, tm, d),
        out_shape=jax.ShapeDtypeStruct((b, t, d), jnp.float32),
        compiler_params=_params(), name="final_norm",
    )(x, mods, g)


def _rope_tables(pos, nb):
    posf = pos.astype(jnp.float32)[:, None]
    lane = jnp.arange(LANES)

    def tab(width, rot):
        half = rot // 2
        inv = ROPE_THETA ** (-jnp.arange(half, dtype=jnp.float32) / half)
        l = lane % width
        ang = posf * inv[l % half][None, :]
        rotated = (l < rot)[None, :]
        sign = jnp.where(l < half, -1.0, 1.0)[None, :]
        cos = jnp.where(rotated, jnp.cos(ang), 1.0)
        sin = jnp.where(rotated, jnp.sin(ang) * sign, 0.0)
        return cos, sin

    cq, sq = tab(HEAD_DIM, HEAD_DIM)
    ci, si = tab(IDX_DIM, IDX_ROPE_DIM)
    tabs = (cq, sq, ci, si)
    if nb > 1:
        tabs = tuple(jnp.tile(a, (nb, 1)) for a in tabs)
    return tabs


def _pad_axis(a, axis, n):
    if a.shape[axis] == n:
        return a
    pad = [(0, 0)] * a.ndim
    pad[axis] = (0, n - a.shape[axis])
    return jnp.pad(a, pad)


def _round_up(n, m):
    return -(-n // m) * m


def _pick(n, prefs):
    for p in prefs:
        if n % p == 0:
            return p
    return n


def _trunk(x, mods, mods_final, past_len, caches, wts):
    b, t, d = x.shape
    depth = len(wts["post"])
    if t >= 512:
        nb, tm = 1, _pick(t, (512, 256, 128))
    else:
        nb, tm = b, t
    pos = past_len + jnp.arange(t, dtype=jnp.int32)
    tabs = _rope_tables(pos, nb)
    n_keys = past_len + t
    topk = min(TOPK_CAP, n_keys // 4)
    if t >= 512:
        tq_a, tq_b, tk = _pick(t, (128,)), _pick(t, (256, 128)), _pick(n_keys, (512, 256, 128))
    else:
        tq_a, tq_b, tk = t, t, 256
    lk = _round_up(n_keys, tk)
    bf = _MXU

    outs = {k: [] for k in ("a_k", "a_v", "a_ki", "b_k", "b_v", "b_lf")}
    for i in range(depth):
        j = i // 2
        if i % 2 == 0:
            qh, kl, vl, kil, kh, vh, kib, qih, wi = _a_in(x, mods[i], wts["g_mix"][i], wts["a_in"][j], tabs, nb, tm)
            outs["a_k"].append(kl.reshape(b, t, N_KV_A, HEAD_DIM))
            outs["a_v"].append(vl.reshape(b, t, N_KV_A, HEAD_DIM))
            outs["a_ki"].append(kil)
            if caches is not None:
                pk = jnp.transpose(caches[0][j], (0, 2, 1, 3)).astype(bf)
                pv = jnp.transpose(caches[1][j], (0, 2, 1, 3)).astype(bf)
                kh = jnp.concatenate([pk, kh], axis=2)
                vh = jnp.concatenate([pv, vh], axis=2)
                kib = jnp.concatenate([caches[2][j].astype(bf), kib], axis=1)
            kh, vh, kib = _pad_axis(kh, 2, lk), _pad_axis(vh, 2, lk), _pad_axis(kib, 1, lk)
            o = _dsa(qh, kh, vh, qih, kib, wi, tq_a, tk, past_len, topk)
            wo = wts["a_out"][j]
        else:
            qh, kl, vl, kh, vh, gate, lf = _b_in(x, mods[i], wts["g_mix"][i], wts["b_in"][j], wts["b_forget"][j], nb, tm)
            outs["b_k"].append(kl.reshape(b, t, N_HEADS_B, HEAD_DIM))
            outs["b_v"].append(vl.reshape(b, t, N_HEADS_B, HEAD_DIM))
            outs["b_lf"].append(lf)
            lf_all = lf
            if caches is not None:
                pk = jnp.transpose(caches[3][j], (0, 2, 1, 3)).astype(bf)
                pv = jnp.transpose(caches[4][j], (0, 2, 1, 3)).astype(bf)
                kh = jnp.concatenate([pk, kh], axis=2)
                vh = jnp.concatenate([pv, vh], axis=2)
                lf_all = jnp.concatenate([caches[5][j].astype(jnp.float32), lf], axis=1)
            kh, vh = _pad_axis(kh, 2, lk), _pad_axis(vh, 2, lk)
            lcum = _round_up(lk, _CUM_BLK)
            cum = _cumsum(_pad_axis(jnp.transpose(lf_all, (0, 2, 1)), 2, lcum))[:, :, :lk]
            cq = cum[:, :, past_len:past_len + t, None]
            ck = cum[:, :, None, :]
            o = _fox(qh, kh, vh, gate, cq, ck, tq_b, tk, past_len)
            wo = wts["b_out"][j]
        w1, w2 = wts["post"][i]
        x = _post(x, o, mods[i], wts["g_ffn"][i], wo, w1, w2, nb, tm)
    y = _final(x, mods_final, wts["g_final"], nb, tm)
    return y, tuple(jnp.stack(outs[k]) for k in ("a_k", "a_v", "a_ki", "b_k", "b_v", "b_lf"))


def kernel(x_prompt, x_sample, cache_a_k, cache_a_v, cache_a_kidx, cache_b_k, cache_b_v, cache_b_logf,
           c_prompt, c_sample, w_ada, b_ada, g_mix, g_ffn, w_in_a, w_out_a, w_in_b, b_forget, w_out_b,
           w_ffn_in, w_ffn_out, w_ada_final, b_ada_final, g_final):
    depth, d = w_ada.shape[0], w_ada.shape[1]
    bp, bs = x_prompt.shape[0], x_sample.shape[0]
    bf = _MXU

    c_all = jnp.concatenate([c_prompt, c_sample], axis=0)
    mods = _ada(c_all, w_ada, b_ada[:, None, :], _pick(6 * d, (1536, 1024, 512, 128)))
    mods = mods.reshape(depth, bp + bs, 6, d)
    mods_f = _ada(c_all, w_ada_final[None], b_ada_final[None, None, :], _pick(2 * d, (1024, 512, 128)))
    mods_f = mods_f.reshape(bp + bs, 2, d)

    qa, kva = N_HEADS_A * HEAD_DIM, N_KV_A * HEAD_DIM
    qia = N_IDX_HEADS * IDX_DIM
    a_in = []
    for j in range(w_in_a.shape[0]):
        w = w_in_a[j]
        o1, o2, o3, o4 = qa, qa + kva, qa + 2 * kva, qa + 2 * kva + qia
        wkw = _pad_axis(w[:, o4:], 1, LANES)
        a_in.append(tuple(a.astype(bf) for a in (w[:, :o1], w[:, o1:o2], w[:, o2:o3], w[:, o3:o4], wkw)))
    inner = N_HEADS_B * HEAD_DIM
    b_in = []
    for j in range(w_in_b.shape[0]):
        w = w_in_b[j]
        wf = _pad_axis(w[:, 4 * inner:], 1, LANES)
        b_in.append(tuple(a.astype(bf) for a in (w[:, :inner], w[:, inner:2 * inner], w[:, 2 * inner:3 * inner],
                                                 w[:, 3 * inner:4 * inner], wf)))
    wts = {
        "g_mix": [g_mix[i][None, :] for i in range(depth)],
        "g_ffn": [g_ffn[i][None, :] for i in range(depth)],
        "g_final": g_final[None, :],
        "a_in": a_in,
        "b_in": b_in,
        "a_out": [w_out_a[j].astype(bf) for j in range(w_out_a.shape[0])],
        "b_out": [w_out_b[j].astype(bf) for j in range(w_out_b.shape[0])],
        "b_forget": [b_forget[j][None, :] for j in range(b_forget.shape[0])],
        "post": [(w_ffn_in[i].astype(bf), w_ffn_out[i].astype(bf)) for i in range(depth)],
    }

    y_p, (pak, pav, paki, pbk, pbv, pblf) = _trunk(
        x_prompt, mods[:, :bp], mods_f[:bp], 0, None, wts)
    y_s, (sak, sav, saki, sbk, sbv, sblf) = _trunk(
        x_sample, mods[:, bp:], mods_f[bp:], cache_a_k.shape[2],
        (cache_a_k, cache_a_v, cache_a_kidx, cache_b_k, cache_b_v, cache_b_logf), wts)
    return (y_p, y_s, pak, pav, paki, pbk, pbv, pblf, sak, sav, saki, sbk, sbv, sblf)
```

```python
import functools
import math

import jax
import jax.numpy as jnp
from jax import lax
from jax.experimental import pallas as pl
from jax.experimental.pallas import tpu as pltpu

CHUNK = 64
HEAD_DIM = 64
N_HEADS_A = 16
N_KV_A = 4
GQA_REP = N_HEADS_A // N_KV_A
N_IDX_HEADS = 8
IDX_DIM = 64
IDX_ROPE_DIM = 32
TOPK_CAP = 256
N_HEADS_B = 16
ROPE_THETA = 10000.0
EPS = 1e-6
WI_SCALE = N_IDX_HEADS ** -0.5 * IDX_DIM ** -0.5
LOG2E = math.log2(math.e)
Q_SCALE = HEAD_DIM ** -0.5 * LOG2E

LANES = 128
SUBLANES = 8
V_ROWS = HEAD_DIM + 16
_MXU = jnp.bfloat16
_NEG = -1e30
_INT_MIN = -(2 ** 31)
_VMEM_LIMIT = 50 * 1024 * 1024


def _dot(a, b):
    return jnp.dot(a, b, preferred_element_type=jnp.float32)


def _dot_nt(a, b):
    return lax.dot_general(a, b, (((1,), (1,)), ((), ())), preferred_element_type=jnp.float32)


def _params():
    return pltpu.CompilerParams(vmem_limit_bytes=_VMEM_LIMIT)


def _const_spec(shape):
    nd = len(shape)
    return pl.BlockSpec(shape, lambda *_: (0,) * nd)


def _ada_kernel(c_ref, w_ref, b_ref, o_ref):
    c = c_ref[...]
    cs = c * jax.nn.sigmoid(c)
    o_ref[...] = jnp.dot(cs, w_ref[...], preferred_element_type=jnp.float32,
                         precision=lax.Precision.HIGHEST) + b_ref[...]


def _ada(c, w, b, tn):
    nl, d, n = w.shape
    r = c.shape[0]
    return pl.pallas_call(
        _ada_kernel,
        grid=(nl, n // tn),
        in_specs=[
            pl.BlockSpec((r, d), lambda l, j: (0, 0)),
            pl.BlockSpec((None, d, tn), lambda l, j: (l, 0, j)),
            pl.BlockSpec((None, 1, tn), lambda l, j: (l, 0, j)),
        ],
        out_specs=pl.BlockSpec((None, r, tn), lambda l, j: (l, 0, j)),
        out_shape=jax.ShapeDtypeStruct((nl, r, n), jnp.float32),
        compiler_params=_params(),
        name="ada",
    )(c, w, b)


def _norm_mod(x, g, shift, scale):
    ms = jnp.mean(x * x, axis=-1, keepdims=True)
    y = x * lax.rsqrt(ms + EPS) * g
    return y * (1.0 + scale) + shift


def _rope128(xc, cos, sin, first, sh):
    partner = jnp.where(first, pltpu.roll(xc, LANES - sh, 1), pltpu.roll(xc, sh, 1))
    return xc * cos + partner * sin


def _store_vt(vt_ref, v, nb, tm, n_heads):
    vt = v.T
    row = lax.broadcasted_iota(jnp.int32, (V_ROWS - HEAD_DIM, tm), 0)
    tail = jnp.where(row == 0, 1.0, 0.0).astype(vt_ref.dtype)
    for bi in range(nb):
        for h in range(n_heads):
            blk = vt[h * HEAD_DIM:(h + 1) * HEAD_DIM, bi * tm:(bi + 1) * tm]
            vt_ref[bi, h, 0:HEAD_DIM, :] = blk.astype(vt_ref.dtype)
            vt_ref[bi, h, HEAD_DIM:V_ROWS, :] = tail


def _row_specs(nb, tm, width):
    return pl.BlockSpec((nb, tm, width), lambda b, t: (b, t, 0))


def _head_spec(nb, nh, tm):
    return pl.BlockSpec((nb, nh, tm, HEAD_DIM), lambda b, t: (b, 0, t, 0))


def _vt_spec(nb, nh, tm):
    return pl.BlockSpec((nb, nh, V_ROWS, tm), lambda b, t: (b, 0, 0, t))


def _tab_spec(nb, tm):
    if nb == 1:
        return pl.BlockSpec((tm, LANES), lambda b, t: (t, 0))
    return pl.BlockSpec((nb * tm, LANES), lambda b, t: (0, 0))


def _a_in_kernel(x_ref, m_ref, g_ref, wq_ref, wk_ref, wv_ref, wqi_ref, wkw_ref,
                 cq_ref, sq_ref, ci_ref, si_ref,
                 qh_ref, kl_ref, vl_ref, kil_ref, kh_ref, vt_ref, kib_ref, qih_ref, wit_ref):
    nb, tm, d = x_ref.shape
    m = nb * tm
    h = _norm_mod(x_ref[...], g_ref[...], m_ref[:, 0:1, :], m_ref[:, 1:2, :])
    hb = h.reshape(m, d).astype(_MXU)
    lane = lax.broadcasted_iota(jnp.int32, (m, LANES), 1)
    first_q = (lane % HEAD_DIM) < (HEAD_DIM // 2)
    first_i = (lane % IDX_ROPE_DIM) < (IDX_ROPE_DIM // 2)
    cq, sq, ci, si = cq_ref[...], sq_ref[...], ci_ref[...], si_ref[...]

    q = _dot(hb, wq_ref[...])
    for j in range(N_HEADS_A // 2):
        c = _rope128(q[:, j * LANES:(j + 1) * LANES], cq, sq, first_q, HEAD_DIM // 2) * Q_SCALE
        qh_ref[:, 2 * j] = c[:, :HEAD_DIM].reshape(nb, tm, HEAD_DIM).astype(qh_ref.dtype)
        qh_ref[:, 2 * j + 1] = c[:, HEAD_DIM:].reshape(nb, tm, HEAD_DIM).astype(qh_ref.dtype)

    k = _dot(hb, wk_ref[...])
    for j in range(N_KV_A // 2):
        c = _rope128(k[:, j * LANES:(j + 1) * LANES], cq, sq, first_q, HEAD_DIM // 2)
        kl_ref[:, :, j * LANES:(j + 1) * LANES] = c.reshape(nb, tm, LANES)
        kh_ref[:, 2 * j] = c[:, :HEAD_DIM].reshape(nb, tm, HEAD_DIM).astype(kh_ref.dtype)
        kh_ref[:, 2 * j + 1] = c[:, HEAD_DIM:].reshape(nb, tm, HEAD_DIM).astype(kh_ref.dtype)

    v = _dot(hb, wv_ref[...])
    vl_ref[...] = v.reshape(nb, tm, N_KV_A * HEAD_DIM)
    _store_vt(vt_ref, v, nb, tm, N_KV_A)

    qi = _dot(hb, wqi_ref[...])
    for j in range(N_IDX_HEADS // 2):
        c = _rope128(qi[:, j * LANES:(j + 1) * LANES], ci, si, first_i, IDX_ROPE_DIM // 2)
        qih_ref[:, 2 * j] = c[:, :IDX_DIM].reshape(nb, tm, IDX_DIM).astype(qih_ref.dtype)
        qih_ref[:, 2 * j + 1] = c[:, IDX_DIM:].reshape(nb, tm, IDX_DIM).astype(qih_ref.dtype)

    kw = _dot(hb, wkw_ref[...])
    is_ki = lane < IDX_DIM
    kw = _rope128(kw, jnp.where(is_ki, ci, 1.0), jnp.where(is_ki, si, 0.0), first_i, IDX_ROPE_DIM // 2)
    ki = kw[:, :IDX_DIM].reshape(nb, tm, IDX_DIM)
    kil_ref[...] = ki
    kib_ref[...] = ki.astype(kib_ref.dtype)
    wit = kw.T[IDX_DIM:IDX_DIM + N_IDX_HEADS, :] * WI_SCALE
    for bi in range(nb):
        wit_ref[bi] = wit[:, bi * tm:(bi + 1) * tm]


def _a_in(x, mods, g, wts, tabs, nb, tm):
    b, t, d = x.shape
    wq, wk, wv, wqi, wkw = wts
    f32, bf = jnp.float32, _MXU
    kvw = N_KV_A * HEAD_DIM
    out_shape = (
        jax.ShapeDtypeStruct((b, N_HEADS_A, t, HEAD_DIM), bf),
        jax.ShapeDtypeStruct((b, t, kvw), f32),
        jax.ShapeDtypeStruct((b, t, kvw), f32),
        jax.ShapeDtypeStruct((b, t, IDX_DIM), f32),
        jax.ShapeDtypeStruct((b, N_KV_A, t, HEAD_DIM), bf),
        jax.ShapeDtypeStruct((b, N_KV_A, V_ROWS, t), bf),
        jax.ShapeDtypeStruct((b, t, IDX_DIM), bf),
        jax.ShapeDtypeStruct((b, N_IDX_HEADS, t, IDX_DIM), bf),
        jax.ShapeDtypeStruct((b, N_IDX_HEADS, t), f32),
    )
    out_specs = (
        _head_spec(nb, N_HEADS_A, tm), _row_specs(nb, tm, kvw), _row_specs(nb, tm, kvw),
        _row_specs(nb, tm, IDX_DIM), _head_spec(nb, N_KV_A, tm), _vt_spec(nb, N_KV_A, tm),
        _row_specs(nb, tm, IDX_DIM), _head_spec(nb, N_IDX_HEADS, tm),
        pl.BlockSpec((nb, N_IDX_HEADS, tm), lambda bi, ti: (bi, 0, ti)),
    )
    in_specs = [
        _row_specs(nb, tm, d),
        pl.BlockSpec((nb, 6, d), lambda bi, ti: (bi, 0, 0)),
        _const_spec(g.shape),
        _const_spec(wq.shape), _const_spec(wk.shape), _const_spec(wv.shape),
        _const_spec(wqi.shape), _const_spec(wkw.shape),
        _tab_spec(nb, tm), _tab_spec(nb, tm), _tab_spec(nb, tm), _tab_spec(nb, tm),
    ]
    return pl.pallas_call(
        _a_in_kernel, grid=(b // nb, t // tm), in_specs=in_specs, out_specs=out_specs,
        out_shape=out_shape, compiler_params=_params(), name="a_in",
    )(x, mods, g, wq, wk, wv, wqi, wkw, *tabs)


def _sort_key(s):
    bits = pltpu.bitcast(s, jnp.int32)
    key = bits ^ ((bits >> 31) & 0x7FFFFFFF)
    return jnp.where(s == 0.0, 0, key)


_CNT_ROWS = 4 * SUBLANES


def _dsa_kernel(qh_ref, kh_ref, vt_ref, qih_ref, ki_ref, wit_ref, o_ref,
                sk_ref, m_ref, acc_ref, ot_ref, *, tq, tk, q_off, topk, idx_bits):
    lk = sk_ref.shape[0]
    q0 = q_off + pl.program_id(1) * tq
    qpos = q0 + lax.broadcasted_iota(jnp.int32, (1, tq), 1)
    n_adm = (qpos // CHUNK + 1) * CHUNK
    n_adm_max = ((q0 + tq - 1) // CHUNK + 1) * CHUNK
    nkb = jnp.minimum((n_adm_max + tk - 1) // tk, lk // tk)
    kf = jnp.float32(topk)

    def key_idx(ks):
        return ks + lax.broadcasted_iota(jnp.int32, (tk, tq), 0)

    qi = qih_ref[0].reshape(N_IDX_HEADS * tq, IDX_DIM)
    wit = wit_ref[0]

    def score_body(kb, carry):
        ks = pl.multiple_of(kb * tk, tk)
        r = jnp.maximum(_dot_nt(ki_ref[0, pl.ds(ks, tk), :], qi), 0.0)
        s = r[:, 0:tq] * wit[0:1, :]
        for h in range(1, N_IDX_HEADS):
            s = s + r[:, h * tq:(h + 1) * tq] * wit[h:h + 1, :]
        sk_ref[pl.ds(ks, tk), :] = jnp.where(key_idx(ks) < n_adm, _sort_key(s), _INT_MIN)
        return carry

    lax.fori_loop(0, nkb, score_body, 0)

    def count(pred):
        def body(kb, acc):
            ks = pl.multiple_of(kb * tk, tk)
            hit = jnp.where(pred(sk_ref[pl.ds(ks, tk), :], ks), 1.0, 0.0)
            return acc + jnp.sum(hit.reshape(tk // _CNT_ROWS, _CNT_ROWS, tq), axis=0)
        acc = lax.fori_loop(0, nkb, body, jnp.zeros((_CNT_ROWS, tq), jnp.float32))
        return jnp.sum(acc, axis=0, keepdims=True)

    c0 = count(lambda blk, ks: blk >= 0)
    thr = jnp.where(c0 >= kf, 0, _INT_MIN).astype(jnp.int32)

    def bit_body(i, thr):
        cand = thr | lax.shift_left(jnp.int32(1), 30 - i)
        c = count(lambda blk, ks: blk >= cand)
        return jnp.where(c >= kf, cand, thr)

    thr = lax.fori_loop(0, 31, bit_body, thr)
    thr = jnp.maximum(thr, _INT_MIN + 1)
    c_ge = count(lambda blk, ks: blk >= thr)

    @pl.when(jnp.max(c_ge) > kf)
    def _():
        c_gt = count(lambda blk, ks: blk > thr)
        need = jnp.where(c_ge > kf, kf - c_gt, jnp.float32(3e38))

        def idx_body(i, x):
            cand = x | lax.shift_left(jnp.int32(1), idx_bits - 1 - i)
            c = count(lambda blk, ks: (blk == thr) & (key_idx(ks) < cand))
            return jnp.where(c < need, cand, x)

        last = lax.fori_loop(0, idx_bits, idx_body, jnp.zeros((1, tq), jnp.int32))

        def drop_body(kb, carry):
            ks = pl.multiple_of(kb * tk, tk)
            blk = sk_ref[pl.ds(ks, tk), :]
            sk_ref[pl.ds(ks, tk), :] = jnp.where((blk == thr) & (key_idx(ks) > last), _INT_MIN, blk)
            return carry

        lax.fori_loop(0, nkb, drop_body, 0)

    m_ref[...] = jnp.full(m_ref.shape, _NEG, jnp.float32)
    acc_ref[...] = jnp.zeros(acc_ref.shape, jnp.float32)

    def attn_body(kb, carry):
        ks = pl.multiple_of(kb * tk, tk)
        bias = jnp.where(sk_ref[pl.ds(ks, tk), :] >= thr, 0.0, _NEG)
        bias = jnp.concatenate([bias] * GQA_REP, axis=1)
        for g in range(N_KV_A):
            q = qh_ref[0, g * GQA_REP:(g + 1) * GQA_REP].reshape(GQA_REP * tq, HEAD_DIM)
            s = _dot_nt(kh_ref[0, g, pl.ds(ks, tk), :], q) + bias
            m_old = m_ref[g]
            m_new = jnp.maximum(m_old, jnp.max(s, axis=0, keepdims=True))
            p = jnp.exp2(s - m_new).astype(_MXU)
            acc_ref[g] = jnp.exp2(m_old - m_new) * acc_ref[g] + _dot(vt_ref[0, g, :, pl.ds(ks, tk)], p)
            m_ref[g] = m_new
        return carry

    lax.fori_loop(0, nkb, attn_body, 0)

    for g in range(N_KV_A):
        acc = acc_ref[g]
        o = acc[0:HEAD_DIM] * (1.0 / acc[HEAD_DIM:HEAD_DIM + 1])
        for r in range(GQA_REP):
            hd = g * GQA_REP + r
            ot_ref[hd * HEAD_DIM:(hd + 1) * HEAD_DIM, :] = o[:, r * tq:(r + 1) * tq]
    o_ref[0] = ot_ref[...].T.astype(o_ref.dtype)


def _dsa(qh, kh, vt, qih, ki, wit, tq, tk, q_off, topk):
    b, _, t, _ = qh.shape
    lk = kh.shape[2]
    kern = functools.partial(_dsa_kernel, tq=tq, tk=tk, q_off=q_off, topk=topk,
                             idx_bits=max(1, (lk - 1).bit_length()))
    return pl.pallas_call(
        kern, grid=(b, t // tq),
        in_specs=[
            pl.BlockSpec((1, N_HEADS_A, tq, HEAD_DIM), lambda bi, qi: (bi, 0, qi, 0)),
            pl.BlockSpec((1, N_KV_A, lk, HEAD_DIM), lambda bi, qi: (bi, 0, 0, 0)),
            pl.BlockSpec((1, N_KV_A, V_ROWS, lk), lambda bi, qi: (bi, 0, 0, 0)),
            pl.BlockSpec((1, N_IDX_HEADS, tq, IDX_DIM), lambda bi, qi: (bi, 0, qi, 0)),
            pl.BlockSpec((1, lk, IDX_DIM), lambda bi, qi: (bi, 0, 0)),
            pl.BlockSpec((1, N_IDX_HEADS, tq), lambda bi, qi: (bi, 0, qi)),
        ],
        out_specs=pl.BlockSpec((1, tq, N_HEADS_A * HEAD_DIM), lambda bi, qi: (bi, qi, 0)),
        out_shape=jax.ShapeDtypeStruct((b, t, N_HEADS_A * HEAD_DIM), _MXU),
        scratch_shapes=[
            pltpu.VMEM((lk, tq), jnp.int32),
            pltpu.VMEM((N_KV_A, 1, GQA_REP * tq), jnp.float32),
            pltpu.VMEM((N_KV_A, V_ROWS, GQA_REP * tq), jnp.float32),
            pltpu.VMEM((N_HEADS_A * HEAD_DIM, tq), jnp.float32),
        ],
        compiler_params=_params(), name="dsa_attn",
    )(qh, kh, vt, qih, ki, wit)


def _b_in_kernel(x_ref, m_ref, g_ref, wq_ref, wk_ref, wv_ref, wg_ref, wf_ref, bf_ref,
                 qh_ref, kl_ref, vl_ref, kh_ref, vt_ref, gate_ref, lf_ref):
    nb, tm, d = x_ref.shape
    m = nb * tm
    h = _norm_mod(x_ref[...], g_ref[...], m_ref[:, 0:1, :], m_ref[:, 1:2, :])
    hb = h.reshape(m, d).astype(_MXU)

    q = _dot(hb, wq_ref[...]) * Q_SCALE
    for hd in range(N_HEADS_B):
        qh_ref[:, hd] = q[:, hd * HEAD_DIM:(hd + 1) * HEAD_DIM].reshape(nb, tm, HEAD_DIM).astype(qh_ref.dtype)
    k = _dot(hb, wk_ref[...])
    kl_ref[...] = k.reshape(nb, tm, N_HEADS_B * HEAD_DIM)
    for hd in range(N_HEADS_B):
        kh_ref[:, hd] = k[:, hd * HEAD_DIM:(hd + 1) * HEAD_DIM].reshape(nb, tm, HEAD_DIM).astype(kh_ref.dtype)
    v = _dot(hb, wv_ref[...])
    vl_ref[...] = v.reshape(nb, tm, N_HEADS_B * HEAD_DIM)
    _store_vt(vt_ref, v, nb, tm, N_HEADS_B)
    gate_ref[...] = _dot(hb, wg_ref[...]).reshape(nb, tm, N_HEADS_B * HEAD_DIM)
    f = _dot(hb, wf_ref[...])[:, :N_HEADS_B] + bf_ref[...]
    lf = jnp.minimum(f, 0.0) - jnp.log1p(jnp.exp(-jnp.abs(f)))
    lf_ref[...] = lf.reshape(nb, tm, N_HEADS_B)


def _b_in(x, mods, g, wts, bf, nb, tm):
    b, t, d = x.shape
    wq, wk, wv, wg, wf = wts
    f32, bft = jnp.float32, _MXU
    inner = N_HEADS_B * HEAD_DIM
    out_shape = (
        jax.ShapeDtypeStruct((b, N_HEADS_B, t, HEAD_DIM), bft),
        jax.ShapeDtypeStruct((b, t, inner), f32),
        jax.ShapeDtypeStruct((b, t, inner), f32),
        jax.ShapeDtypeStruct((b, N_HEADS_B, t, HEAD_DIM), bft),
        jax.ShapeDtypeStruct((b, N_HEADS_B, V_ROWS, t), bft),
        jax.ShapeDtypeStruct((b, t, inner), f32),
        jax.ShapeDtypeStruct((b, t, N_HEADS_B), f32),
    )
    out_specs = (
        _head_spec(nb, N_HEADS_B, tm), _row_specs(nb, tm, inner), _row_specs(nb, tm, inner),
        _head_spec(nb, N_HEADS_B, tm), _vt_spec(nb, N_HEADS_B, tm), _row_specs(nb, tm, inner),
        _row_specs(nb, tm, N_HEADS_B),
    )
    in_specs = [
        _row_specs(nb, tm, d),
        pl.BlockSpec((nb, 6, d), lambda bi, ti: (bi, 0, 0)),
        _const_spec(g.shape),
        _const_spec(wq.shape), _const_spec(wk.shape), _const_spec(wv.shape),
        _const_spec(wg.shape), _const_spec(wf.shape), _const_spec(bf.shape),
    ]
    return pl.pallas_call(
        _b_in_kernel, grid=(b // nb, t // tm), in_specs=in_specs, out_specs=out_specs,
        out_shape=out_shape, compiler_params=_params(), name="b_in",
    )(x, mods, g, wq, wk, wv, wg, wf, bf)


_CUM_BLK = 256


def _cumsum_kernel(x_ref, o_ref):
    n = x_ref.shape[2]
    r = lax.broadcasted_iota(jnp.int32, (_CUM_BLK, _CUM_BLK), 0)
    c = lax.broadcasted_iota(jnp.int32, (_CUM_BLK, _CUM_BLK), 1)
    tri = jnp.where(r <= c, 1.0, 0.0)
    carry = jnp.zeros((x_ref.shape[1], 1), jnp.float32)
    for j in range(n // _CUM_BLK):
        xs = x_ref[0, :, j * _CUM_BLK:(j + 1) * _CUM_BLK]
        cs = jnp.dot(xs, tri, preferred_element_type=jnp.float32, precision=lax.Precision.HIGHEST) + carry
        o_ref[0, :, j * _CUM_BLK:(j + 1) * _CUM_BLK] = cs
        carry = cs[:, _CUM_BLK - 1:_CUM_BLK]


def _cumsum(x):
    b, h, n = x.shape
    return pl.pallas_call(
        _cumsum_kernel, grid=(b,),
        in_specs=[pl.BlockSpec((1, h, n), lambda bi: (bi, 0, 0))],
        out_specs=pl.BlockSpec((1, h, n), lambda bi: (bi, 0, 0)),
        out_shape=jax.ShapeDtypeStruct((b, h, n), jnp.float32),
        compiler_params=_params(), name="cumsum",
    )(x)


_FOX_HPB = 2


def _fox_kernel(qh_ref, kh_ref, vt_ref, g_ref, cum_ref, cumt_ref, o_ref, ckb_ref, acc_ref,
                *, tq, tk, q_off):
    lk = kh_ref.shape[2]
    hi = pl.program_id(1)
    q0 = q_off + pl.program_id(2) * tq
    nkb = jnp.minimum((q0 + tq + tk - 1) // tk, lk // tk)
    n_full = jnp.minimum((q0 + 1) // tk, nkb)
    qpos = q0 + lax.broadcasted_iota(jnp.int32, (1, tq), 1)

    @pl.when(pl.program_id(2) == 0)
    def _():
        for hp in range(N_HEADS_B // _FOX_HPB):
            @pl.when(hi == hp)
            def _(hp=hp):
                for r in range(_FOX_HPB):
                    hd = hp * _FOX_HPB + r

                    def fill(j, carry, r=r, hd=hd):
                        rs = pl.multiple_of(j * tk, tk)
                        col = cum_ref[0, pl.ds(rs, tk), hd:hd + 1] * LOG2E
                        ckb_ref[r, pl.ds(rs, tk), :] = jnp.broadcast_to(col, (tk, LANES))
                        return carry

                    lax.fori_loop(0, lk // tk, fill, 0)

    outs = []
    for r in range(_FOX_HPB):
        q = qh_ref[0, r]
        cq = cumt_ref[0, pl.ds(hi * _FOX_HPB + r, 1), :] * LOG2E
        acc_ref[r] = jnp.zeros(acc_ref.shape[1:], jnp.float32)

        def step(kb, m_old, masked, r=r, q=q, cq=cq):
            ks = pl.multiple_of(kb * tk, tk)
            ck = ckb_ref[r, pl.ds(ks, tk), :]
            s = _dot_nt(kh_ref[0, r, pl.ds(ks, tk), :], q) - jnp.concatenate([ck] * (tq // LANES), axis=1)
            if masked:
                kidx = ks + lax.broadcasted_iota(jnp.int32, (tk, tq), 0)
                s = jnp.where(kidx <= qpos, s, _NEG)
            m_new = jnp.maximum(m_old, jnp.max(s, axis=0, keepdims=True) + cq)
            p = jnp.exp2(s + (cq - m_new)).astype(_MXU)
            acc_ref[r] = jnp.exp2(m_old - m_new) * acc_ref[r] + _dot(vt_ref[0, r, :, pl.ds(ks, tk)], p)
            return m_new

        m0 = jnp.full((1, tq), _NEG, jnp.float32)
        m1 = lax.fori_loop(0, n_full, functools.partial(step, masked=False), m0)
        lax.fori_loop(n_full, nkb, functools.partial(step, masked=True), m1)
        acc = acc_ref[r]
        outs.append(acc[0:HEAD_DIM] * (1.0 / acc[HEAD_DIM:HEAD_DIM + 1]))
    o = jnp.concatenate(outs, axis=0).T
    o_ref[0] = (o * jax.nn.sigmoid(g_ref[0])).astype(o_ref.dtype)


def _fox(qh, kh, vt, gate, cum, cumt, tq, tk, q_off):
    b, nh, t, _ = qh.shape
    lk = kh.shape[2]
    hp = _FOX_HPB
    qblk0 = q_off // tq
    kern = functools.partial(_fox_kernel, tq=tq, tk=tk, q_off=q_off)
    return pl.pallas_call(
        kern, grid=(b, nh // hp, t // tq),
        in_specs=[
            pl.BlockSpec((1, hp, tq, HEAD_DIM), lambda bi, hi, qi: (bi, hi, qi, 0)),
            pl.BlockSpec((1, hp, lk, HEAD_DIM), lambda bi, hi, qi: (bi, hi, 0, 0)),
            pl.BlockSpec((1, hp, V_ROWS, lk), lambda bi, hi, qi: (bi, hi, 0, 0)),
            pl.BlockSpec((1, tq, hp * HEAD_DIM), lambda bi, hi, qi: (bi, qi, hi)),
            pl.BlockSpec((1, lk, nh), lambda bi, hi, qi: (bi, 0, 0)),
            pl.BlockSpec((1, nh, tq), lambda bi, hi, qi: (bi, 0, qblk0 + qi)),
        ],
        out_specs=pl.BlockSpec((1, tq, hp * HEAD_DIM), lambda bi, hi, qi: (bi, qi, hi)),
        out_shape=jax.ShapeDtypeStruct((b, t, nh * HEAD_DIM), _MXU),
        scratch_shapes=[
            pltpu.VMEM((hp, lk, LANES), jnp.float32),
            pltpu.VMEM((hp, V_ROWS, tq), jnp.float32),
        ],
        compiler_params=_params(), name="fox_attn",
    )(qh, kh, vt, gate, cum, cumt)


def _post_kernel(x_ref, o_ref, m_ref, g_ref, wo_ref, w1_ref, w2_ref, y_ref, *, n_ff_chunks):
    nb, tm, d = x_ref.shape
    m = nb * tm
    d_ff = w2_ref.shape[0]
    cw = d_ff // n_ff_chunks
    att = _dot(o_ref[...].reshape(m, o_ref.shape[2]), wo_ref[...]).reshape(nb, tm, d)
    x1 = x_ref[...] + m_ref[:, 2:3, :] * att
    h = _norm_mod(x1, g_ref[...], m_ref[:, 3:4, :], m_ref[:, 4:5, :]).reshape(m, d).astype(_MXU)
    acc = None
    for j in range(n_ff_chunks):
        gate = _dot(h, w1_ref[:, j * cw:(j + 1) * cw])
        up = _dot(h, w1_ref[:, d_ff + j * cw:d_ff + (j + 1) * cw])
        act = (gate * jax.nn.sigmoid(gate) * up).astype(_MXU)
        part = _dot(act, w2_ref[j * cw:(j + 1) * cw, :])
        acc = part if acc is None else acc + part
    y_ref[...] = x1 + m_ref[:, 5:6, :] * acc.reshape(nb, tm, d)


def _post(x, o, mods, g, wo, w1, w2, nb, tm):
    b, t, d = x.shape
    d_ff = w2.shape[0]
    n_chunks = 2 if (d_ff // 2) % LANES == 0 else 1
    single = pl.Buffered(1)
    return pl.pallas_call(
        functools.partial(_post_kernel, n_ff_chunks=n_chunks),
        grid=(b // nb, t // tm),
        in_specs=[
            _row_specs(nb, tm, d),
            _row_specs(nb, tm, o.shape[2]),
            pl.BlockSpec((nb, 6, d), lambda bi, ti: (bi, 0, 0)),
            _const_spec(g.shape),
            pl.BlockSpec(wo.shape, lambda bi, ti: (0, 0), pipeline_mode=single),
            pl.BlockSpec(w1.shape, lambda bi, ti: (0, 0), pipeline_mode=single),
            pl.BlockSpec(w2.shape, lambda bi, ti: (0, 0), pipeline_mode=single),
        ],
        out_specs=_row_specs(nb, tm, d),
        out_shape=jax.ShapeDtypeStruct((b, t, d), jnp.float32),
        compiler_params=_params(), name="post",
    )(x, o, mods, g, wo, w1, w2)


def _final_kernel(x_ref, m_ref, g_ref, y_ref):
    y_ref[...] = _norm_mod(x_ref[...], g_ref[...], m_ref[:, 0:1, :], m_ref[:, 1:2, :])


def _final(x, mods, g, nb, tm):
    b, t, d = x.shape
    return pl.pallas_call(
        _final_kernel, grid=(b // nb, t // tm),
        in_specs=[_row_specs(nb, tm, d), pl.BlockSpec((nb, 2, d), lambda bi, ti: (bi, 0, 0)),
                  _const_spec(g.shape)],
        out_specs=_row_specs(nb, tm, d),
        out_shape=jax.ShapeDtypeStruct((b, t, d), jnp.float32),
        compiler_params=_params(), name="final_norm",
    )(x, mods, g)


def _rope_tables(pos, nb):
    posf = pos.astype(jnp.float32)[:, None]
    lane = jnp.arange(LANES)

    def tab(width, rot):
        half = rot // 2
        inv = ROPE_THETA ** (-jnp.arange(half, dtype=jnp.float32) / half)
        l = lane % width
        ang = posf * inv[l % half][None, :]
        rotated = (l < rot)[None, :]
        sign = jnp.where(l < half, -1.0, 1.0)[None, :]
        cos = jnp.where(rotated, jnp.cos(ang), 1.0)
        sin = jnp.where(rotated, jnp.sin(ang) * sign, 0.0)
        return cos, sin

    cq, sq = tab(HEAD_DIM, HEAD_DIM)
    ci, si = tab(IDX_DIM, IDX_ROPE_DIM)
    tabs = (cq, sq, ci, si)
    if nb > 1:
        tabs = tuple(jnp.tile(a, (nb, 1)) for a in tabs)
    return tabs


def _pad_axis(a, axis, n):
    if a.shape[axis] == n:
        return a
    pad = [(0, 0)] * a.ndim
    pad[axis] = (0, n - a.shape[axis])
    return jnp.pad(a, pad)


def _round_up(n, m):
    return -(-n // m) * m


def _pick(n, prefs):
    for p in prefs:
        if n % p == 0:
            return p
    return n


def _past_vt(pv):
    b, p, h, _ = pv.shape
    vt = jnp.transpose(pv, (0, 2, 3, 1)).astype(_MXU)
    tail = jnp.zeros((b, h, V_ROWS - HEAD_DIM, p), _MXU).at[:, :, 0, :].set(1.0)
    return jnp.concatenate([vt, tail], axis=2)


def _trunk(x, mods, mods_final, past_len, caches, wts):
    b, t, d = x.shape
    depth = len(wts["post"])
    if t >= 512:
        nb, tm = 1, _pick(t, (512, 256, 128))
    else:
        nb, tm = b, t
    pos = past_len + jnp.arange(t, dtype=jnp.int32)
    tabs = _rope_tables(pos, nb)
    n_keys = past_len + t
    topk = min(TOPK_CAP, n_keys // 4)
    tp = _round_up(t, LANES)
    if t >= 512:
        tq_a, tq_b, tk = LANES, _pick(tp, (512, 256, 128)), _pick(n_keys, (512, 256, 128))
    else:
        tq_a, tq_b, tk = LANES, LANES, 256
    lk = _round_up(max(n_keys, past_len + tp), tk)
    bf = _MXU

    outs = {k: [] for k in ("a_k", "a_v", "a_ki", "b_k", "b_v", "b_lf")}
    for i in range(depth):
        j = i // 2
        if i % 2 == 0:
            qh, kl, vl, kil, kh, vt, kib, qih, wit = _a_in(x, mods[i], wts["g_mix"][i], wts["a_in"][j], tabs, nb, tm)
            outs["a_k"].append(kl.reshape(b, t, N_KV_A, HEAD_DIM))
            outs["a_v"].append(vl.reshape(b, t, N_KV_A, HEAD_DIM))
            outs["a_ki"].append(kil)
            if caches is not None:
                kh = jnp.concatenate([jnp.transpose(caches[0][j], (0, 2, 1, 3)).astype(bf), kh], axis=2)
                vt = jnp.concatenate([_past_vt(caches[1][j]), vt], axis=3)
                kib = jnp.concatenate([caches[2][j].astype(bf), kib], axis=1)
            kh, vt, kib = _pad_axis(kh, 2, lk), _pad_axis(vt, 3, lk), _pad_axis(kib, 1, lk)
            qh, qih, wit = _pad_axis(qh, 2, tp), _pad_axis(qih, 2, tp), _pad_axis(wit, 2, tp)
            o = _dsa(qh, kh, vt, qih, kib, wit, tq_a, tk, past_len, topk)[:, :t]
            wo = wts["a_out"][j]
        else:
            qh, kl, vl, kh, vt, gate, lf = _b_in(x, mods[i], wts["g_mix"][i], wts["b_in"][j], wts["b_forget"][j], nb, tm)
            outs["b_k"].append(kl.reshape(b, t, N_HEADS_B, HEAD_DIM))
            outs["b_v"].append(vl.reshape(b, t, N_HEADS_B, HEAD_DIM))
            outs["b_lf"].append(lf)
            lf_all = lf
            if caches is not None:
                kh = jnp.concatenate([jnp.transpose(caches[3][j], (0, 2, 1, 3)).astype(bf), kh], axis=2)
                vt = jnp.concatenate([_past_vt(caches[4][j]), vt], axis=3)
                lf_all = jnp.concatenate([caches[5][j].astype(jnp.float32), lf], axis=1)
            kh, vt = _pad_axis(kh, 2, lk), _pad_axis(vt, 3, lk)
            lcum = _round_up(lk, _CUM_BLK)
            cumt = _cumsum(_pad_axis(jnp.transpose(lf_all, (0, 2, 1)), 2, lcum))[:, :, :lk]
            cum = jnp.transpose(cumt, (0, 2, 1))
            qh, gate = _pad_axis(qh, 2, tp), _pad_axis(gate, 1, tp)
            o = _fox(qh, kh, vt, gate, cum, cumt, tq_b, tk, past_len)[:, :t]
            wo = wts["b_out"][j]
        w1, w2 = wts["post"][i]
        x = _post(x, o, mods[i], wts["g_ffn"][i], wo, w1, w2, nb, tm)
    y = _final(x, mods_final, wts["g_final"], nb, tm)
    return y, tuple(jnp.stack(outs[k]) for k in ("a_k", "a_v", "a_ki", "b_k", "b_v", "b_lf"))


def kernel(x_prompt, x_sample, cache_a_k, cache_a_v, cache_a_kidx, cache_b_k, cache_b_v, cache_b_logf,
           c_prompt, c_sample, w_ada, b_ada, g_mix, g_ffn, w_in_a, w_out_a, w_in_b, b_forget, w_out_b,
           w_ffn_in, w_ffn_out, w_ada_final, b_ada_final, g_final):
    depth, d = w_ada.shape[0], w_ada.shape[1]
    bp, bs = x_prompt.shape[0], x_sample.shape[0]
    bf = _MXU

    c_all = jnp.concatenate([c_prompt, c_sample], axis=0)
    mods = _ada(c_all, w_ada, b_ada[:, None, :], _pick(6 * d, (1536, 1024, 512, 128)))
    mods = mods.reshape(depth, bp + bs, 6, d)
    mods_f = _ada(c_all, w_ada_final[None], b_ada_final[None, None, :], _pick(2 * d, (1024, 512, 128)))
    mods_f = mods_f.reshape(bp + bs, 2, d)

    qa, kva = N_HEADS_A * HEAD_DIM, N_KV_A * HEAD_DIM
    qia = N_IDX_HEADS * IDX_DIM
    a_in = []
    for j in range(w_in_a.shape[0]):
        w = w_in_a[j]
        o1, o2, o3, o4 = qa, qa + kva, qa + 2 * kva, qa + 2 * kva + qia
        wkw = _pad_axis(w[:, o4:], 1, LANES)
        a_in.append(tuple(a.astype(bf) for a in (w[:, :o1], w[:, o1:o2], w[:, o2:o3], w[:, o3:o4], wkw)))
    inner = N_HEADS_B * HEAD_DIM
    b_in = []
    for j in range(w_in_b.shape[0]):
        w = w_in_b[j]
        wf = _pad_axis(w[:, 4 * inner:], 1, LANES)
        b_in.append(tuple(a.astype(bf) for a in (w[:, :inner], w[:, inner:2 * inner], w[:, 2 * inner:3 * inner],
                                                 w[:, 3 * inner:4 * inner], wf)))
    wts = {
        "g_mix": [g_mix[i][None, :] for i in range(depth)],
        "g_ffn": [g_ffn[i][None, :] for i in range(depth)],
        "g_final": g_final[None, :],
        "a_in": a_in,
        "b_in": b_in,
        "a_out": [w_out_a[j].astype(bf) for j in range(w_out_a.shape[0])],
        "b_out": [w_out_b[j].astype(bf) for j in range(w_out_b.shape[0])],
        "b_forget": [b_forget[j][None, :] for j in range(b_forget.shape[0])],
        "post": [(w_ffn_in[i].astype(bf), w_ffn_out[i].astype(bf)) for i in range(depth)],
    }

    y_p, (pak, pav, paki, pbk, pbv, pblf) = _trunk(
        x_prompt, mods[:, :bp], mods_f[:bp], 0, None, wts)
    y_s, (sak, sav, saki, sbk, sbv, sblf) = _trunk(
        x_sample, mods[:, bp:], mods_f[bp:], cache_a_k.shape[2],
        (cache_a_k, cache_a_v, cache_a_kidx, cache_b_k, cache_b_v, cache_b_logf), wts)
    return (y_p, y_s, pak, pav, paki, pbk, pbv, pblf, sak, sav, saki, sbk, sbv, sblf)
```
